```python
import jax, jax.numpy as jnp
from jax import lax
import numpy as np

D_MODEL = 1024
BATCH = 8
SEQ = 2048
DEPTH = 4
DEC_BATCH = 8
DEC_SEQ = 16
PAST_LEN = 4096

CHUNK = 64
GMLP_WIDTH = 1024
GMLP_HEADS = 8
GMLP_HEAD_DIM = GMLP_WIDTH // GMLP_HEADS
GMLP_CHUNK = 128
SSD_WIDTH = 1024
SSD_HEAD_DIM = 64
SSD_HEADS = SSD_WIDTH // SSD_HEAD_DIM
SSD_GROUPS = 2
SSD_HEADS_PER_GROUP = SSD_HEADS // SSD_GROUPS
SSD_STATE = 128
SSD_CONV = 4
SSD_CHUNK = 128
SSD_CONV_DIM = SSD_WIDTH + 2 * SSD_GROUPS * SSD_STATE
MIX_WIDTH = GMLP_WIDTH + SSD_WIDTH
D_IN_PROJ = 2 * GMLP_WIDTH + SSD_WIDTH + SSD_CONV_DIM + SSD_HEADS
D_FF = 4 * D_MODEL
ALPHA = (2 * DEPTH) ** 0.25
BETA = (8 * DEPTH) ** -0.25
LN_EPS = 1e-5
RMS_EPS = 1e-5

kernel_name = "hymba_gmlp_ssd_deepnorm_stream"


def layer_norm(x, g, b):
    xf = x.astype(jnp.float32)
    mu = jnp.mean(xf, axis=-1, keepdims=True)
    var = jnp.mean(jnp.square(xf - mu), axis=-1, keepdims=True)
    y = (xf - mu) * lax.rsqrt(var + LN_EPS) * g.astype(jnp.float32) + b.astype(jnp.float32)
    return y.astype(x.dtype)


def gated_rms_norm(y, z, g):
    bsz, L, _ = y.shape
    h = y.astype(jnp.float32) * jax.nn.silu(z.astype(jnp.float32))
    h = h.reshape(bsz, L, SSD_GROUPS, SSD_WIDTH // SSD_GROUPS)
    h = h * lax.rsqrt(jnp.mean(jnp.square(h), axis=-1, keepdims=True) + RMS_EPS)
    return (h.reshape(bsz, L, SSD_WIDTH) * g.astype(jnp.float32)).astype(z.dtype)


def gmlp_mix(u, v, ln_g, ln_b, w_s, b_s, offset):
    bsz, L, _ = v.shape
    u = jax.nn.gelu(u)
    v = layer_norm(jax.nn.gelu(v), ln_g, ln_b)
    q = min(L, GMLP_CHUNK)
    causal = jnp.tril(jnp.ones((GMLP_CHUNK, GMLP_CHUNK), dtype=bool))
    w = jnp.where(causal, w_s, jnp.zeros_like(w_s))[:, offset:offset + q, offset:offset + q]
    bq = b_s[:, offset:offset + q]
    vc = v.reshape(bsz, L // q, q, GMLP_HEADS, GMLP_HEAD_DIM)
    s = jnp.einsum('hts,bcshd->bcthd', w, vc) + bq.T[None, None, :, :, None]
    return u * s.reshape(bsz, L, GMLP_WIDTH), v


def causal_conv(xbc, conv_state, w, b):
    L = xbc.shape[1]
    xp = jnp.concatenate([conv_state.astype(xbc.dtype), xbc], axis=1)
    out = b + w[0] * xp[:, 0:L]
    for k in range(1, SSD_CONV):
        out = out + w[k] * xp[:, k:k + L]
    return jax.nn.silu(out), xp[:, -(SSD_CONV - 1):]


def ssd_scan(x, dt, a, bm, cm, h0):
    f32 = jnp.float32
    bsz, L = x.shape[:2]
    q = min(L, SSD_CHUNK)
    nc = L // q
    G, R, P, N = SSD_GROUPS, SSD_HEADS_PER_GROUP, SSD_HEAD_DIM, SSD_STATE
    xf = x.astype(f32).reshape(bsz, nc, q, G, R, P)
    dtc = dt.reshape(bsz, nc, q, G, R)
    bc = bm.astype(f32).reshape(bsz, nc, q, G, N)
    cc = cm.astype(f32).reshape(bsz, nc, q, G, N)
    acum = jnp.cumsum(dtc * a.reshape(G, R), axis=2)
    seg = acum[:, :, :, None] - acum[:, :, None, :]
    causal = jnp.tril(jnp.ones((q, q), dtype=bool))[:, :, None, None]
    decay = jnp.exp(jnp.where(causal, seg, -jnp.inf))
    cb = jnp.einsum('bcign,bcjgn->bcijg', cc, bc)
    y_diag = jnp.einsum('bcijg,bcijgr,bcjgr,bcjgrp->bcigrp', cb, decay, dtc, xf)
    dec_end = jnp.exp(acum[:, :, -1:] - acum)
    st = jnp.einsum('bcjgn,bcjgr,bcjgrp->bcgrpn', bc, dec_end * dtc, xf)
    block_decay = jnp.exp(acum[:, :, -1])

    def step(h, inp):
        s_c, d_c = inp
        return d_c[..., None, None] * h + s_c, h

    h_init = h0.astype(f32).reshape(bsz, G, R, P, N)
    h_final, h_starts = lax.scan(step, h_init, (jnp.moveaxis(st, 1, 0), jnp.moveaxis(block_decay, 1, 0)))
    h_starts = jnp.moveaxis(h_starts, 0, 1)
    y_off = jnp.einsum('bcign,bcgrpn,bcigr->bcigrp', cc, h_starts, jnp.exp(acum))
    y = (y_diag + y_off).reshape(bsz, L, SSD_HEADS, P)
    return y, h_final.reshape(bsz, SSD_HEADS, P, N)


def mixer(x, ssm_state, conv_state, offset, w_in, gmlp_ln_g, gmlp_ln_b, gmlp_ws, gmlp_bs,
          conv_w, conv_b, dt_bias, a_log, d_skip, ssd_norm_g, w_out):
    bsz, L, _ = x.shape
    proj = jnp.einsum('bld,de->ble', x, w_in)
    s1 = GMLP_WIDTH
    s2 = 2 * GMLP_WIDTH
    s3 = s2 + SSD_WIDTH
    s4 = s3 + SSD_CONV_DIM
    u, v, z, xbc, dt = proj[..., :s1], proj[..., s1:s2], proj[..., s2:s3], proj[..., s3:s4], proj[..., s4:]
    y_g, v_rows = gmlp_mix(u, v, gmlp_ln_g, gmlp_ln_b, gmlp_ws, gmlp_bs, offset)
    xbc, new_conv = causal_conv(xbc, conv_state, conv_w, conv_b)
    xs = xbc[..., :SSD_WIDTH].reshape(bsz, L, SSD_HEADS, SSD_HEAD_DIM)
    bm = xbc[..., SSD_WIDTH:SSD_WIDTH + SSD_GROUPS * SSD_STATE].reshape(bsz, L, SSD_GROUPS, SSD_STATE)
    cm = xbc[..., SSD_WIDTH + SSD_GROUPS * SSD_STATE:].reshape(bsz, L, SSD_GROUPS, SSD_STATE)
    dtp = jax.nn.softplus(dt.astype(jnp.float32) + dt_bias.astype(jnp.float32))
    a = -jnp.exp(a_log.astype(jnp.float32))
    y, new_ssm = ssd_scan(xs, dtp, a, bm, cm, ssm_state)
    y = y + d_skip.astype(jnp.float32)[:, None] * xs.astype(jnp.float32)
    y_s = gated_rms_norm(y.reshape(bsz, L, SSD_WIDTH), z, ssd_norm_g)
    out = jnp.einsum('ble,ed->bld', jnp.concatenate([y_g, y_s], axis=-1), w_out)
    return out, new_ssm.astype(ssm_state.dtype), new_conv, v_rows


def squared_relu_ffn(x, w1, w2):
    h = jnp.square(jax.nn.relu(jnp.einsum('bld,df->blf', x, w1)))
    return jnp.einsum('blf,fd->bld', h, w2)


def setup_inputs(seed: int = 0) -> dict:
    key = jax.random.key(seed)
    ks = jax.random.split(key, 24)
    f32 = jnp.float32
    nrm = lambda k, shp: jax.random.normal(k, shp, dtype=f32)
    dt0 = jnp.exp(jax.random.uniform(ks[9], (DEPTH, SSD_HEADS), minval=np.log(1e-3), maxval=np.log(1e-1)))
    return {
        'x_prompt': nrm(ks[0], (BATCH, SEQ, D_MODEL)),
        'x_sample': nrm(ks[1], (DEC_BATCH, DEC_SEQ, D_MODEL)),
        'state_ssm': 0.3 * nrm(ks[2], (DEPTH, DEC_BATCH, SSD_HEADS, SSD_HEAD_DIM, SSD_STATE)),
        'state_conv': nrm(ks[3], (DEPTH, DEC_BATCH, SSD_CONV - 1, SSD_CONV_DIM)),
        'w_in': nrm(ks[4], (DEPTH, D_MODEL, D_IN_PROJ)) * D_MODEL ** -0.5,
        'gmlp_ln_g': 1.0 + 0.02 * nrm(ks[5], (DEPTH, GMLP_WIDTH)),
        'gmlp_ln_b': 0.02 * nrm(ks[6], (DEPTH, GMLP_WIDTH)),
        'gmlp_ws': nrm(ks[7], (DEPTH, GMLP_HEADS, GMLP_CHUNK, GMLP_CHUNK)) * GMLP_CHUNK ** -0.5,
        'gmlp_bs': 1.0 + 0.02 * nrm(ks[8], (DEPTH, GMLP_HEADS, GMLP_CHUNK)),
        'conv_w': nrm(ks[10], (DEPTH, SSD_CONV, SSD_CONV_DIM)) * SSD_CONV ** -0.5,
        'conv_b': 0.02 * nrm(ks[11], (DEPTH, SSD_CONV_DIM)),
        'dt_bias': dt0 + jnp.log(-jnp.expm1(-dt0)),
        'a_log': jnp.log(jax.random.uniform(ks[12], (DEPTH, SSD_HEADS), minval=1.0, maxval=16.0)),
        'd_skip': 1.0 + 0.02 * nrm(ks[13], (DEPTH, SSD_HEADS)),
        'ssd_norm_g': 1.0 + 0.02 * nrm(ks[14], (DEPTH, SSD_WIDTH)),
        'w_out': nrm(ks[15], (DEPTH, MIX_WIDTH, D_MODEL)) * (MIX_WIDTH ** -0.5 * BETA),
        'ln1_g': 1.0 + 0.02 * nrm(ks[16], (DEPTH, D_MODEL)),
        'ln1_b': 0.02 * nrm(ks[17], (DEPTH, D_MODEL)),
        'w_ff1': nrm(ks[18], (DEPTH, D_MODEL, D_FF)) * (D_MODEL ** -0.5 * BETA),
        'w_ff2': nrm(ks[19], (DEPTH, D_FF, D_MODEL)) * (D_FF ** -0.5 * BETA),
        'ln2_g': 1.0 + 0.02 * nrm(ks[20], (DEPTH, D_MODEL)),
        'ln2_b': 0.02 * nrm(ks[21], (DEPTH, D_MODEL)),
    }


def reference(x_prompt, x_sample, state_ssm, state_conv, w_in, gmlp_ln_g, gmlp_ln_b, gmlp_ws, gmlp_bs,
              conv_w, conv_b, dt_bias, a_log, d_skip, ssd_norm_g, w_out, ln1_g, ln1_b, w_ff1, w_ff2,
              ln2_g, ln2_b):
    bp = x_prompt.shape[0]
    zero_ssm = jnp.zeros((bp, SSD_HEADS, SSD_HEAD_DIM, SSD_STATE), x_prompt.dtype)
    zero_conv = jnp.zeros((bp, SSD_CONV - 1, SSD_CONV_DIM), x_prompt.dtype)
    sample_offset = PAST_LEN % GMLP_CHUNK
    xp, xs = x_prompt, x_sample
    ssm_p, conv_p, ssm_s, conv_s, v_s = [], [], [], [], []
    for l in range(DEPTH):
        lw = (w_in[l], gmlp_ln_g[l], gmlp_ln_b[l], gmlp_ws[l], gmlp_bs[l], conv_w[l], conv_b[l],
              dt_bias[l], a_log[l], d_skip[l], ssd_norm_g[l], w_out[l])
        hp, sp, cp, _ = mixer(xp, zero_ssm, zero_conv, 0, *lw)
        hs, ss, cs, vs = mixer(xs, state_ssm[l], state_conv[l], sample_offset, *lw)
        xp = layer_norm(ALPHA * xp + hp, ln1_g[l], ln1_b[l])
        xs = layer_norm(ALPHA * xs + hs, ln1_g[l], ln1_b[l])
        xp = layer_norm(ALPHA * xp + squared_relu_ffn(xp, w_ff1[l], w_ff2[l]), ln2_g[l], ln2_b[l])
        xs = layer_norm(ALPHA * xs + squared_relu_ffn(xs, w_ff1[l], w_ff2[l]), ln2_g[l], ln2_b[l])
        ssm_p.append(sp)
        conv_p.append(cp)
        ssm_s.append(ss)
        conv_s.append(cs)
        v_s.append(vs)
    return (xp, xs, jnp.stack(ssm_p), jnp.stack(conv_p), jnp.stack(ssm_s), jnp.stack(conv_s), jnp.stack(v_s))
```

```python
import functools

import jax
import jax.numpy as jnp
from jax import lax
from jax.experimental import pallas as pl
from jax.experimental.pallas import tpu as pltpu

LANES = 128
SUBLANES = 8
VMEM_LIMIT_BYTES = 56 * 1024 * 1024

D_MODEL = 1024
GMLP_WIDTH = 1024
GMLP_HEADS = 8
GMLP_HEAD_DIM = GMLP_WIDTH // GMLP_HEADS
CHUNK = 128
SSD_WIDTH = 1024
SSD_HEAD_DIM = 64
SSD_HEADS = SSD_WIDTH // SSD_HEAD_DIM
SSD_GROUPS = 2
SSD_HEADS_PER_GROUP = SSD_HEADS // SSD_GROUPS
SSD_STATE = 128
SSD_CONV = 4
SSD_CONV_DIM = SSD_WIDTH + 2 * SSD_GROUPS * SSD_STATE
MIX_WIDTH = GMLP_WIDTH + SSD_WIDTH
D_FF = 4 * D_MODEL
LN_EPS = 1e-5
RMS_EPS = 1e-5

OFF_U = 0
OFF_V = GMLP_WIDTH
OFF_Z = 2 * GMLP_WIDTH
OFF_XBC = OFF_Z + SSD_WIDTH
OFF_DT = OFF_XBC + SSD_CONV_DIM
D_IN_PAD = OFF_DT + LANES

BF16 = jnp.bfloat16
F32 = jnp.float32


def _dot(a, b):
    return jnp.dot(a, b, preferred_element_type=F32)


def _dot_nt(a, b):
    return lax.dot_general(a, b, (((1,), (1,)), ((), ())), preferred_element_type=F32)


def _layer_norm(x, g, b):
    mu = jnp.mean(x, axis=-1, keepdims=True)
    xc = x - mu
    var = jnp.mean(xc * xc, axis=-1, keepdims=True)
    return xc * lax.rsqrt(var + LN_EPS) * g + b


def _silu(x):
    return x * jax.nn.sigmoid(x)


def _softplus(x):
    return jnp.maximum(x, 0.0) + jnp.log1p(jnp.exp(-jnp.abs(x)))


def _split3(x):
    hi = x.astype(BF16)
    r1 = x - hi.astype(F32)
    mid = r1.astype(BF16)
    lo = (r1 - mid.astype(F32)).astype(BF16)
    return hi, mid, lo


def _mixer_kernel(*refs, tile, valid, has_state, alpha):
    q = CHUNK
    it = iter(refs)
    x_ref = next(it)
    if has_state:
        ssm0_ref = next(it)
        conv0_ref = next(it)
    w_in_ref = next(it)
    w_out_ref = next(it)
    gln_g_ref = next(it)
    gln_b_ref = next(it)
    ws_ref = next(it)
    bst_ref = next(it)
    cw_ref = next(it)
    cb_ref = next(it)
    dtb_ref = next(it)
    alog_ref = next(it)
    dsk_ref = next(it)
    ng_ref = next(it)
    ln1g_ref = next(it)
    ln1b_ref = next(it)
    xo_ref = next(it)
    ssm_ref = next(it)
    conv_ref = next(it)
    if has_state:
        vout_ref = next(it)
    u_s = next(it)
    v_s = next(it)
    z_s = next(it)
    xe_s = next(it)
    xc_s = next(it)
    dt_s = next(it)
    mix_s = next(it)
    st_s = next(it)

    l = pl.program_id(1)
    n_l = pl.num_programs(1)
    halo = SUBLANES

    @pl.when(l == 0)
    def _init():
        xe_s[0:halo, :] = jnp.zeros((halo, SSD_CONV_DIM), F32)
        if has_state:
            xe_s[halo - (SSD_CONV - 1):halo, :] = conv0_ref[0]
            for k in range(SSD_WIDTH // LANES):
                st_s[:, k * LANES:(k + 1) * LANES] = ssm0_ref[0, k * LANES:(k + 1) * LANES, :].T
        else:
            st_s[...] = jnp.zeros(st_s.shape, F32)

    xb = x_ref[0].astype(BF16)
    u_s[...] = _dot(xb, w_in_ref[:, OFF_U:OFF_V])
    v_s[...] = _dot(xb, w_in_ref[:, OFF_V:OFF_Z])
    z_s[...] = _dot(xb, w_in_ref[:, OFF_Z:OFF_XBC])
    xe_s[halo:halo + tile, :] = _dot(xb, w_in_ref[:, OFF_XBC:OFF_DT])
    dt_s[...] = _dot(xb, w_in_ref[:, OFF_DT:D_IN_PAD])

    for rb in range(tile // q):
        r = rb * q
        acc = cb_ref[...] + cw_ref[0:1, :] * xe_s[halo - 3 + r:halo - 3 + r + q, :]
        for k in range(1, SSD_CONV):
            o = halo - 3 + k + r
            acc = acc + cw_ref[k:k + 1, :] * xe_s[o:o + q, :]
        xc_s[r:r + q, :] = _silu(acc)

    row_i = lax.broadcasted_iota(jnp.int32, (q, q), 0)
    col_i = lax.broadcasted_iota(jnp.int32, (q, q), 1)
    tri = row_i >= col_i
    low_half = col_i < SSD_HEAD_DIM
    tril_b = jnp.where(tri, 1.0, 0.0).astype(BF16)
    a_neg = -jnp.exp(alog_ref[...])

    def chunk(c, carry):
        r0 = pl.multiple_of(c * q, q)
        rows = pl.ds(r0, q)

        u = jax.nn.gelu(u_s[rows, :])
        v = _layer_norm(jax.nn.gelu(v_s[rows, :]), gln_g_ref[...], gln_b_ref[...])
        if has_state:
            vout_ref[0] = v[0:valid, :]
        vb = v.astype(BF16)
        for h in range(GMLP_HEADS):
            cs = slice(h * GMLP_HEAD_DIM, (h + 1) * GMLP_HEAD_DIM)
            wm = jnp.where(tri, ws_ref[h], jnp.zeros((q, q), BF16))
            s = _dot(wm, vb[:, cs]) + bst_ref[:, h:h + 1]
            mix_s[rows, cs] = (u[:, cs] * s).astype(BF16)

        dtp = _softplus(dt_s[rows, :] + dtb_ref[...])
        if valid < q:
            dtp = jnp.where(row_i < valid, dtp, 0.0)
        d_a = dtp * a_neg
        hi, mid, lo = _split3(d_a)
        acum = _dot(tril_b, hi) + _dot(tril_b, mid) + _dot(tril_b, lo)
        acum_t = acum.T
        dtp_t = dtp.T
        eac = jnp.exp(acum)
        last_t = acum_t[:, q - 1:q]
        w3_t = dtp_t * jnp.exp(last_t - acum_t)
        bdec_t = jnp.exp(last_t)

        ypieces = []
        for g in range(SSD_GROUPS):
            bcol = SSD_WIDTH + g * SSD_STATE
            ccol = SSD_WIDTH + SSD_GROUPS * SSD_STATE + g * SSD_STATE
            bc = xc_s[rows, bcol:bcol + SSD_STATE]
            cc = xc_s[rows, ccol:ccol + SSD_STATE]
            cbm = _dot_nt(cc.astype(BF16), bc.astype(BF16))
            bc_t = bc.T
            for k in range(SSD_HEADS_PER_GROUP // 2):
                h0 = g * SSD_HEADS_PER_GROUP + 2 * k
                cs = slice(h0 * SSD_HEAD_DIM, (h0 + 2) * SSD_HEAD_DIM)
                xp = xc_s[rows, cs]
                x_top = jnp.where(low_half, xp, 0.0).astype(BF16)
                x_bot = jnp.where(low_half, 0.0, xp).astype(BF16)
                stp = st_s[:, cs]
                s_top = jnp.where(low_half, stp, 0.0).astype(BF16)
                s_bot = jnp.where(low_half, 0.0, stp).astype(BF16)
                lhs = []
                for h in (h0, h0 + 1):
                    seg = acum[:, h:h + 1] - acum_t[h:h + 1, :]
                    dec = jnp.exp(jnp.where(tri, seg, -jnp.inf))
                    lhs.append((cbm * dec * dtp_t[h:h + 1, :]).astype(BF16))
                for h in (h0, h0 + 1):
                    lhs.append((cc * eac[:, h:h + 1]).astype(BF16))
                yp = _dot(jnp.concatenate(lhs, axis=1),
                          jnp.concatenate([x_top, x_bot, s_top, s_bot], axis=0))
                ypieces.append(yp + dsk_ref[:, cs] * xp)
                l2 = jnp.concatenate([(bc_t * w3_t[h0:h0 + 1, :]).astype(BF16),
                                      (bc_t * w3_t[h0 + 1:h0 + 2, :]).astype(BF16)], axis=1)
                bd = jnp.where(low_half, bdec_t[h0:h0 + 1, :], bdec_t[h0 + 1:h0 + 2, :])
                st_s[:, cs] = bd * stp + _dot(l2, jnp.concatenate([x_top, x_bot], axis=0))
        y = jnp.concatenate(ypieces, axis=1)

        hh = y * _silu(z_s[rows, :])
        gw = SSD_WIDTH // SSD_GROUPS
        parts = []
        for g in range(SSD_GROUPS):
            hg = hh[:, g * gw:(g + 1) * gw]
            ms = jnp.mean(hg * hg, axis=-1, keepdims=True)
            parts.append(hg * lax.rsqrt(ms + RMS_EPS))
        ys = jnp.concatenate(parts, axis=1) * ng_ref[...]
        mix_s[rows, GMLP_WIDTH:MIX_WIDTH] = ys.astype(BF16)
        return carry

    lax.fori_loop(0, tile // q, chunk, 0)

    o = _dot(mix_s[...], w_out_ref[...])
    xo_ref[0] = _layer_norm(alpha * x_ref[0] + o, ln1g_ref[...], ln1b_ref[...])

    @pl.when(l == n_l - 1)
    def _emit():
        conv_ref[0] = xe_s[halo + valid - (SSD_CONV - 1):halo + valid, :]
        for k in range(SSD_WIDTH // LANES):
            ssm_ref[0, k * LANES:(k + 1) * LANES, :] = st_s[:, k * LANES:(k + 1) * LANES].T

    @pl.when(l < n_l - 1)
    def _carry():
        xe_s[0:halo, :] = xe_s[tile:tile + halo, :]


def _const_spec(shape):
    nd = len(shape)
    return pl.BlockSpec(shape, lambda b, l, _nd=nd: (0,) * _nd, pipeline_mode=pl.Buffered(1))


def _mixer_call(x, ssm0, conv0, lw, *, tile, valid, alpha):
    bsz, seq, _ = x.shape
    has_state = ssm0 is not None
    n_l = seq // tile
    args = [x]
    in_specs = [pl.BlockSpec((1, tile, D_MODEL), lambda b, l: (b, l, 0))]
    if has_state:
        args += [ssm0, conv0]
        in_specs += [pl.BlockSpec((1, SSD_WIDTH, SSD_STATE), lambda b, l: (b, 0, 0)),
                     pl.BlockSpec((1, SSD_CONV - 1, SSD_CONV_DIM), lambda b, l: (b, 0, 0))]
    for w in lw:
        args.append(w)
        in_specs.append(_const_spec(w.shape))
    out_shape = [jax.ShapeDtypeStruct((bsz, seq, D_MODEL), F32),
                 jax.ShapeDtypeStruct((bsz, SSD_WIDTH, SSD_STATE), F32),
                 jax.ShapeDtypeStruct((bsz, SSD_CONV - 1, SSD_CONV_DIM), F32)]
    out_specs = [pl.BlockSpec((1, tile, D_MODEL), lambda b, l: (b, l, 0)),
                 pl.BlockSpec((1, SSD_WIDTH, SSD_STATE), lambda b, l: (b, 0, 0)),
                 pl.BlockSpec((1, SSD_CONV - 1, SSD_CONV_DIM), lambda b, l: (b, 0, 0))]
    if has_state:
        out_shape.append(jax.ShapeDtypeStruct((bsz, valid, GMLP_WIDTH), F32))
        out_specs.append(pl.BlockSpec((1, valid, GMLP_WIDTH), lambda b, l: (b, 0, 0)))
    scratch = [
        pltpu.VMEM((tile, GMLP_WIDTH), F32),
        pltpu.VMEM((tile, GMLP_WIDTH), F32),
        pltpu.VMEM((tile, SSD_WIDTH), F32),
        pltpu.VMEM((tile + SUBLANES, SSD_CONV_DIM), F32),
        pltpu.VMEM((tile, SSD_CONV_DIM), F32),
        pltpu.VMEM((tile, LANES), F32),
        pltpu.VMEM((tile, MIX_WIDTH), BF16),
        pltpu.VMEM((SSD_STATE, SSD_WIDTH), F32),
    ]
    kern = functools.partial(_mixer_kernel, tile=tile, valid=valid, has_state=has_state, alpha=alpha)
    return pl.pallas_call(
        kern,
        grid=(bsz, n_l),
        in_specs=in_specs,
        out_specs=out_specs,
        out_shape=out_shape,
        scratch_shapes=scratch,
        compiler_params=pltpu.CompilerParams(
            dimension_semantics=("parallel", "arbitrary"),
            vmem_limit_bytes=VMEM_LIMIT_BYTES),
        name="mixer_state" if has_state else "mixer",
    )(*args)


def _ffn_kernel(x_ref, w1_ref, w2_ref, g_ref, b_ref, o_ref, *, alpha, fblk):
    x = x_ref[...]
    xb = x.astype(BF16)
    acc = jnp.zeros(x.shape, F32)
    for j in range(D_FF // fblk):
        h = _dot(xb, w1_ref[:, j * fblk:(j + 1) * fblk])
        h = jnp.square(jnp.maximum(h, 0.0)).astype(BF16)
        acc = acc + _dot(h, w2_ref[j * fblk:(j + 1) * fblk, :])
    o_ref[...] = _layer_norm(alpha * x + acc, g_ref[...], b_ref[...])


def _ffn_call(x2d, w1, w2, g, b, *, tile, alpha):
    rows = x2d.shape[0]
    wspec = lambda shape: pl.BlockSpec(shape, lambda i: (0, 0), pipeline_mode=pl.Buffered(1))
    return pl.pallas_call(
        functools.partial(_ffn_kernel, alpha=alpha, fblk=1024),
        grid=(rows // tile,),
        in_specs=[pl.BlockSpec((tile, D_MODEL), lambda i: (i, 0)),
                  wspec(w1.shape), wspec(w2.shape), wspec(g.shape), wspec(b.shape)],
        out_specs=pl.BlockSpec((tile, D_MODEL), lambda i: (i, 0)),
        out_shape=jax.ShapeDtypeStruct((rows, D_MODEL), F32),
        compiler_params=pltpu.CompilerParams(
            dimension_semantics=("parallel",),
            vmem_limit_bytes=VMEM_LIMIT_BYTES),
        name="ffn",
    )(x2d, w1, w2, g, b)


def _pad_lanes(v):
    return jnp.pad(v, (0, LANES - v.shape[0]))[None, :]


def kernel(x_prompt, x_sample, state_ssm, state_conv, w_in, gmlp_ln_g, gmlp_ln_b, gmlp_ws, gmlp_bs,
           conv_w, conv_b, dt_bias, a_log, d_skip, ssd_norm_g, w_out, ln1_g, ln1_b, w_ff1, w_ff2,
           ln2_g, ln2_b):
    depth = w_in.shape[0]
    alpha = float((2 * depth) ** 0.25)
    bp, seq, _ = x_prompt.shape
    bs, dec_seq, _ = x_sample.shape
    tile_p = min(512, seq)
    row_tile = 512

    xp = x_prompt
    xs = jnp.pad(x_sample, ((0, 0), (0, CHUNK - dec_seq), (0, 0)))
    ssm_p, conv_p, ssm_s, conv_s, v_s = [], [], [], [], []
    for l in range(depth):
        w_in_l = jnp.pad(w_in[l], ((0, 0), (0, D_IN_PAD - w_in.shape[2]))).astype(BF16)
        lw = (w_in_l, w_out[l].astype(BF16), gmlp_ln_g[l][None, :], gmlp_ln_b[l][None, :],
              gmlp_ws[l].astype(BF16), gmlp_bs[l].T, conv_w[l], conv_b[l][None, :],
              _pad_lanes(dt_bias[l]), _pad_lanes(a_log[l]),
              jnp.repeat(d_skip[l], SSD_HEAD_DIM)[None, :], ssd_norm_g[l][None, :],
              ln1_g[l][None, :], ln1_b[l][None, :])
        w1 = w_ff1[l].astype(BF16)
        w2 = w_ff2[l].astype(BF16)
        g2 = ln2_g[l][None, :]
        b2 = ln2_b[l][None, :]

        xp, sp, cp = _mixer_call(xp, None, None, lw, tile=tile_p, valid=tile_p, alpha=alpha)
        xs, ss, cs, vs = _mixer_call(xs, state_ssm[l].reshape(bs, SSD_WIDTH, SSD_STATE), state_conv[l], lw,
                                     tile=CHUNK, valid=dec_seq, alpha=alpha)
        xp = _ffn_call(xp.reshape(bp * seq, D_MODEL), w1, w2, g2, b2,
                       tile=min(row_tile, bp * seq), alpha=alpha).reshape(bp, seq, D_MODEL)
        xs = _ffn_call(xs.reshape(bs * CHUNK, D_MODEL), w1, w2, g2, b2,
                       tile=min(row_tile, bs * CHUNK), alpha=alpha).reshape(bs, CHUNK, D_MODEL)
        ssm_p.append(sp.reshape(bp, SSD_HEADS, SSD_HEAD_DIM, SSD_STATE))
        conv_p.append(cp)
        ssm_s.append(ss.reshape(bs, SSD_HEADS, SSD_HEAD_DIM, SSD_STATE))
        conv_s.append(cs)
        v_s.append(vs)
    return (xp, xs[:, :dec_seq], jnp.stack(ssm_p), jnp.stack(conv_p), jnp.stack(ssm_s), jnp.stack(conv_s),
            jnp.stack(v_s))
```

```python
import collections
import functools

import jax
import jax.numpy as jnp
from jax import lax
from jax.experimental import pallas as pl
from jax.experimental.pallas import tpu as pltpu

LANES = 128
SUBLANES = 8
VMEM_LIMIT_BYTES = 56 * 1024 * 1024

D_MODEL = 1024
GMLP_WIDTH = 1024
GMLP_HEADS = 8
GMLP_HEAD_DIM = GMLP_WIDTH // GMLP_HEADS
CHUNK = 128
SSD_WIDTH = 1024
SSD_HEAD_DIM = 64
SSD_HEADS = SSD_WIDTH // SSD_HEAD_DIM
SSD_GROUPS = 2
SSD_HEADS_PER_GROUP = SSD_HEADS // SSD_GROUPS
SSD_STATE = 128
SSD_CONV = 4
SSD_CONV_DIM = SSD_WIDTH + 2 * SSD_GROUPS * SSD_STATE
MIX_WIDTH = GMLP_WIDTH + SSD_WIDTH
D_FF = 4 * D_MODEL
LN_EPS = 1e-5
RMS_EPS = 1e-5
ROW_GROUPS = CHUNK // SUBLANES
HALO = (SSD_CONV - 1) * SUBLANES


def _pos_of_row(p):
    return ROW_GROUPS * (p % SUBLANES) + p // SUBLANES


def _row_of_pos(t):
    return SUBLANES * (t % ROW_GROUPS) + t // ROW_GROUPS

OFF_U = 0
OFF_V = GMLP_WIDTH
OFF_Z = 2 * GMLP_WIDTH
OFF_XBC = OFF_Z + SSD_WIDTH
OFF_DT = OFF_XBC + SSD_CONV_DIM
D_IN_PAD = OFF_DT + LANES

PIPE_TILE = 256

BF16 = jnp.bfloat16
F32 = jnp.float32

MixerParams = collections.namedtuple(
    "MixerParams", "w_in w_out gln_g gln_b ws bst cw cb dtb alog dsk ng ln1g ln1b")
TileBufs = collections.namedtuple("TileBufs", "u v z xe dt")


def _dot(a, b):
    return jnp.dot(a, b, preferred_element_type=F32)


def _dot_nt(a, b):
    return lax.dot_general(a, b, (((1,), (1,)), ((), ())), preferred_element_type=F32)


def _layer_norm(x, g, b):
    mu = jnp.mean(x, axis=-1, keepdims=True)
    xc = x - mu
    var = jnp.mean(xc * xc, axis=-1, keepdims=True)
    return xc * lax.rsqrt(var + LN_EPS) * g + b


def _silu(x):
    return x * jax.nn.sigmoid(x)


def _softplus(x):
    return jnp.maximum(x, 0.0) + jnp.log1p(jnp.exp(-jnp.abs(x)))


def _split3(x):
    hi = x.astype(BF16)
    r1 = x - hi.astype(F32)
    mid = r1.astype(BF16)
    lo = (r1 - mid.astype(F32)).astype(BF16)
    return hi, mid, lo


_GELU_A = -2.0 * 0.7978845608028654 * 1.4426950408889634
_GELU_B = _GELU_A * 0.044715


def _gelu(x):
    return x / (1.0 + jnp.exp2(x * (x * x * _GELU_B + _GELU_A)))


def _chunk_consts(prm):
    q = CHUNK
    row_i = lax.broadcasted_iota(jnp.int32, (q, q), 0)
    col_i = lax.broadcasted_iota(jnp.int32, (q, q), 1)
    sub_bits = SUBLANES.bit_length() - 1
    grp_bits = ROW_GROUPS.bit_length() - 1
    row_pos = ((row_i & (SUBLANES - 1)) << grp_bits) | (row_i >> sub_bits)
    col_pos = ((col_i & (SUBLANES - 1)) << grp_bits) | (col_i >> sub_bits)
    tri = row_pos >= col_pos
    return dict(
        row_pos=row_pos,
        tri=tri,
        sub_last=lax.broadcasted_iota(jnp.int32, (SSD_CONV - 1, SUBLANES, SSD_CONV_DIM), 1) == SUBLANES - 1,
        low_half=col_i < SSD_HEAD_DIM,
        tril_b=jnp.where(tri, 1.0, 0.0).astype(BF16),
        a_neg=-jnp.exp(prm.alog[...]),
    )


PROJ_BLOCK = 512


def _in_proj_tasks(x, prm, bufs):
    rows = x.shape[0]
    xb = x.astype(BF16)
    segs = ((bufs.u, 0, OFF_U, GMLP_WIDTH), (bufs.v, 0, OFF_V, GMLP_WIDTH), (bufs.z, 0, OFF_Z, SSD_WIDTH),
            (bufs.xe, HALO, OFF_XBC, SSD_CONV_DIM), (bufs.dt, 0, OFF_DT, LANES))
    tasks = []
    for ref, row0, off, width in segs:
        for c0 in range(0, width, PROJ_BLOCK):
            c1 = min(c0 + PROJ_BLOCK, width)

            def task(ref=ref, row0=row0, off=off, c0=c0, c1=c1):
                ref[row0:row0 + rows, c0:c1] = _dot(xb, prm.w_in[:, off + c0:off + c1])
            tasks.append(task)
    return tasks


def _out_proj_tasks(mix_ref, x_ref, o_ref, row0, rows, prm, alpha):
    parts = []

    def piece(c0):
        parts.append(_dot(mix_ref[...], prm.w_out[:, c0:c0 + PROJ_BLOCK]))

    def finish():
        o = jnp.concatenate(parts, axis=1)
        x = x_ref[0, row0:row0 + rows, :]
        o_ref[0, row0:row0 + rows, :] = _layer_norm(alpha * x + o, prm.ln1g[...], prm.ln1b[...])

    tasks = [functools.partial(piece, c0) for c0 in range(0, D_MODEL, PROJ_BLOCK)]
    return tasks + [finish]


PIECES_PER_CHUNK = 17


def _chunk_pieces(r, bufs, mix_ref, st_s, prm, cst, *, valid=CHUNK, vout_ref=None, reset=None):
    q = CHUNK
    rows = slice(r, r + q)
    tri, low_half, row_pos = cst["tri"], cst["low_half"], cst["row_pos"]

    u = _gelu(bufs.u[rows, :])
    yield
    v = _layer_norm(_gelu(bufs.v[rows, :]), prm.gln_g[...], prm.gln_b[...])
    if vout_ref is not None:
        vout_ref[0] = v
    vb = v.astype(BF16)
    yield
    for h in range(GMLP_HEADS):
        cs = slice(h * GMLP_HEAD_DIM, (h + 1) * GMLP_HEAD_DIM)
        wm = jnp.where(tri, prm.ws[h], jnp.zeros((q, q), BF16))
        s = _dot(wm, vb[:, cs]) + prm.bst[:, h:h + 1]
        mix_ref[rows, cs] = (u[:, cs] * s).astype(BF16)
        if h % 2 == 1:
            yield

    nk = SSD_CONV - 1
    x3 = bufs.xe[HALO + r:HALO + r + q, :].reshape(ROW_GROUPS, SUBLANES, SSD_CONV_DIM)
    prev3 = bufs.xe[r:HALO + r, :].reshape(nk, SUBLANES, SSD_CONV_DIM)
    lead = pltpu.roll(jnp.where(cst["sub_last"], prev3, x3[ROW_GROUPS - nk:]), 1, 1)
    ext = jnp.concatenate([lead, x3], axis=0)
    acc = prm.cb[...][None] + prm.cw[0:1, :][None] * ext[0:ROW_GROUPS]
    for k in range(1, SSD_CONV):
        acc = acc + prm.cw[k:k + 1, :][None] * ext[k:k + ROW_GROUPS]
    xc = _silu(acc).reshape(q, SSD_CONV_DIM)
    yield

    dtp = _softplus(bufs.dt[rows, :] + prm.dtb[...])
    if valid < q:
        dtp = jnp.where(row_pos < valid, dtp, 0.0)
    d_a = dtp * cst["a_neg"]
    hi, mid, lo = _split3(d_a)
    tril_b = cst["tril_b"]
    acum = _dot(tril_b, hi) + _dot(tril_b, mid) + _dot(tril_b, lo)
    acum_t = acum.T
    dtp_t = dtp.T
    eac = jnp.exp(acum)
    last_t = acum_t[:, q - 1:q]
    w3_t = dtp_t * jnp.exp(last_t - acum_t)
    bdec_t = jnp.exp(last_t)
    yield

    ypieces = []
    for g in range(SSD_GROUPS):
        bcol = SSD_WIDTH + g * SSD_STATE
        ccol = SSD_WIDTH + SSD_GROUPS * SSD_STATE + g * SSD_STATE
        bc = xc[:, bcol:bcol + SSD_STATE]
        cc = xc[:, ccol:ccol + SSD_STATE]
        cbm = _dot_nt(cc.astype(BF16), bc.astype(BF16))
        bc_t = bc.T
        for k in range(SSD_HEADS_PER_GROUP // 2):
            h0 = g * SSD_HEADS_PER_GROUP + 2 * k
            cs = slice(h0 * SSD_HEAD_DIM, (h0 + 2) * SSD_HEAD_DIM)
            xp = xc[:, cs]
            x_top = jnp.where(low_half, xp, 0.0).astype(BF16)
            x_bot = jnp.where(low_half, 0.0, xp).astype(BF16)
            stp = st_s[:, cs]
            if reset is not None:
                stp = jnp.where(reset, 0.0, stp)
            s_top = jnp.where(low_half, stp, 0.0).astype(BF16)
            s_bot = jnp.where(low_half, 0.0, stp).astype(BF16)
            lhs = []
            for h in (h0, h0 + 1):
                seg = acum[:, h:h + 1] - acum_t[h:h + 1, :]
                dec = jnp.exp(jnp.where(tri, seg, -jnp.inf))
                lhs.append((cbm * dec * dtp_t[h:h + 1, :]).astype(BF16))
            for h in (h0, h0 + 1):
                lhs.append((cc * eac[:, h:h + 1]).astype(BF16))
            yp = _dot(jnp.concatenate(lhs, axis=1),
                      jnp.concatenate([x_top, x_bot, s_top, s_bot], axis=0))
            ypieces.append(yp + prm.dsk[:, cs] * xp)
            l2 = jnp.concatenate([(bc_t * w3_t[h0:h0 + 1, :]).astype(BF16),
                                  (bc_t * w3_t[h0 + 1:h0 + 2, :]).astype(BF16)], axis=1)
            bd = jnp.where(low_half, bdec_t[h0:h0 + 1, :], bdec_t[h0 + 1:h0 + 2, :])
            st_s[:, cs] = bd * stp + _dot(l2, jnp.concatenate([x_top, x_bot], axis=0))
            yield
    y = jnp.concatenate(ypieces, axis=1)

    hh = y * _silu(bufs.z[rows, :])
    gw = SSD_WIDTH // SSD_GROUPS
    parts = []
    for g in range(SSD_GROUPS):
        hg = hh[:, g * gw:(g + 1) * gw]
        ms = jnp.mean(hg * hg, axis=-1, keepdims=True)
        parts.append(hg * lax.rsqrt(ms + RMS_EPS))
    ys = jnp.concatenate(parts, axis=1) * prm.ng[...]
    mix_ref[rows, GMLP_WIDTH:MIX_WIDTH] = ys.astype(BF16)
    yield


def _interleave(mxu_tasks, vec_pieces, n_vec):
    done = 0
    for i, task in enumerate(mxu_tasks):
        task()
        while done < (i + 1) * n_vec // len(mxu_tasks):
            next(vec_pieces)
            done += 1
    for _ in vec_pieces:
        raise AssertionError("more vector pieces than announced")


def _emit_state(st_s, xe_ref, chunk_row, valid, ssm_ref, conv_ref):
    for i in range(SSD_CONV - 1):
        row = HALO + chunk_row + _row_of_pos(valid - (SSD_CONV - 1) + i)
        conv_ref[0, i:i + 1, :] = xe_ref[row:row + 1, :]
    for k in range(SSD_WIDTH // LANES):
        ssm_ref[0, k * LANES:(k + 1) * LANES, :] = st_s[:, k * LANES:(k + 1) * LANES].T


def _tile_scratch(rows):
    return [pltpu.VMEM((rows, GMLP_WIDTH), F32),
            pltpu.VMEM((rows, GMLP_WIDTH), F32),
            pltpu.VMEM((rows, SSD_WIDTH), F32),
            pltpu.VMEM((rows + HALO, SSD_CONV_DIM), F32),
            pltpu.VMEM((rows, LANES), F32)]


N_PARAMS = len(MixerParams._fields)
N_BUFS = len(TileBufs._fields)


def _mixer_state_kernel(x_ref, ssm0_ref, conv0_ref, *refs, valid, alpha):
    prm = MixerParams(*refs[:N_PARAMS])
    xo_ref, ssm_ref, conv_ref, vout_ref = refs[N_PARAMS:N_PARAMS + 4]
    scr = refs[N_PARAMS + 4:]
    bufs = TileBufs(*scr[:N_BUFS])
    mix_s, st_s = scr[N_BUFS:]

    bufs.xe[0:HALO, :] = jnp.zeros((HALO, SSD_CONV_DIM), F32)
    for i in range(SSD_CONV - 1):
        row = i * SUBLANES + SUBLANES - 1
        bufs.xe[row:row + 1, :] = conv0_ref[0, i:i + 1, :]
    for k in range(SSD_WIDTH // LANES):
        st_s[:, k * LANES:(k + 1) * LANES] = ssm0_ref[0, k * LANES:(k + 1) * LANES, :].T

    cst = _chunk_consts(prm)
    for task in _in_proj_tasks(x_ref[0], prm, bufs):
        task()
    for _ in _chunk_pieces(0, bufs, mix_s, st_s, prm, cst, valid=valid, vout_ref=vout_ref):
        pass
    for task in _out_proj_tasks(mix_s, x_ref, xo_ref, 0, CHUNK, prm, alpha):
        task()
    _emit_state(st_s, bufs.xe, 0, valid, ssm_ref, conv_ref)


def _mixer_pipe_kernel(xin_ref, xres_ref, *refs, tile, n_l, alpha):
    prm = MixerParams(*refs[:N_PARAMS])
    xo_ref, ssm_ref, conv_ref = refs[N_PARAMS:N_PARAMS + 3]
    scr = refs[N_PARAMS + 3:]
    buf0 = TileBufs(*scr[:N_BUFS])
    buf1 = TileBufs(*scr[N_BUFS:2 * N_BUFS])
    mix0, mix1, st_s = scr[2 * N_BUFS:]
    q = CHUNK
    k = pl.program_id(0)
    starts_seq = lax.rem(2 * k, n_l) == 0

    @pl.when(k == 0)
    def _fill():
        for ref in (*buf1, mix0, st_s):
            ref[...] = jnp.zeros(ref.shape, ref.dtype)

    @pl.when(starts_seq)
    def _zero_halo():
        buf0.xe[0:HALO, :] = jnp.zeros((HALO, SSD_CONV_DIM), F32)

    @pl.when(jnp.logical_not(starts_seq))
    def _carry_halo():
        buf0.xe[0:HALO, :] = buf1.xe[tile:tile + HALO, :]

    cst = _chunk_consts(prm)

    n_chunks = tile // q

    def chunks(bufs, mix_ref, reset):
        for c in range(n_chunks):
            yield from _chunk_pieces(c * q, bufs, mix_ref, st_s, prm, cst, reset=reset if c == 0 else None)

    _interleave(_in_proj_tasks(xin_ref[0, 0:tile, :], prm, buf0)
                + _out_proj_tasks(mix0, xres_ref, xo_ref, 0, tile, prm, alpha),
                chunks(buf1, mix1, None), n_chunks * PIECES_PER_CHUNK)
    _emit_state(st_s, buf1.xe, tile - q, q, ssm_ref, conv_ref)
    buf1.xe[0:HALO, :] = buf0.xe[tile:tile + HALO, :]

    _interleave(_in_proj_tasks(xin_ref[0, tile:2 * tile, :], prm, buf1)
                + _out_proj_tasks(mix1, xres_ref, xo_ref, tile, tile, prm, alpha),
                chunks(buf0, mix0, starts_seq), n_chunks * PIECES_PER_CHUNK)


def _const_spec(shape, n_grid):
    nd = len(shape)
    if n_grid == 1:
        imap = lambda k, _nd=nd: (0,) * _nd
    else:
        imap = lambda b, l, _nd=nd: (0,) * _nd
    return pl.BlockSpec(shape, imap, pipeline_mode=pl.Buffered(1))


def _mixer_pipe_call(x, lw, *, alpha):
    bsz, seq, _ = x.shape
    tile = PIPE_TILE
    blk = 2 * tile
    assert seq % blk == 0
    n_l = seq // tile
    n_blk = bsz * seq // blk
    xb = x.reshape(n_blk, blk, D_MODEL)
    in_specs = [pl.BlockSpec((1, blk, D_MODEL), lambda k: (jnp.minimum(k, n_blk - 1), 0, 0)),
                pl.BlockSpec((1, blk, D_MODEL), lambda k: (jnp.maximum(k - 1, 0), 0, 0))]
    in_specs += [_const_spec(w.shape, 1) for w in lw]
    state_idx = lambda k: (jnp.maximum(2 * k - 1, 0) // n_l, 0, 0)
    out_shape = [jax.ShapeDtypeStruct((n_blk, blk, D_MODEL), F32),
                 jax.ShapeDtypeStruct((bsz, SSD_WIDTH, SSD_STATE), F32),
                 jax.ShapeDtypeStruct((bsz, SSD_CONV - 1, SSD_CONV_DIM), F32)]
    out_specs = [pl.BlockSpec((1, blk, D_MODEL), lambda k: (jnp.maximum(k - 1, 0), 0, 0)),
                 pl.BlockSpec((1, SSD_WIDTH, SSD_STATE), state_idx),
                 pl.BlockSpec((1, SSD_CONV - 1, SSD_CONV_DIM), state_idx)]
    scratch = (_tile_scratch(tile) + _tile_scratch(tile)
               + [pltpu.VMEM((tile, MIX_WIDTH), BF16), pltpu.VMEM((tile, MIX_WIDTH), BF16),
                  pltpu.VMEM((SSD_STATE, SSD_WIDTH), F32)])
    xo, ssm, conv = pl.pallas_call(
        functools.partial(_mixer_pipe_kernel, tile=tile, n_l=n_l, alpha=alpha),
        grid=(n_blk + 1,),
        in_specs=in_specs,
        out_specs=out_specs,
        out_shape=out_shape,
        scratch_shapes=scratch,
        compiler_params=pltpu.CompilerParams(
            dimension_semantics=("arbitrary",),
            vmem_limit_bytes=VMEM_LIMIT_BYTES),
        name="mixer_pipe",
    )(xb, xb, *lw)
    return xo.reshape(bsz, seq, D_MODEL), ssm, conv


def _mixer_state_call(x, ssm0, conv0, lw, *, valid, alpha):
    bsz = x.shape[0]
    per_b = lambda b: (b, 0, 0)
    in_specs = [pl.BlockSpec((1, CHUNK, D_MODEL), per_b),
                pl.BlockSpec((1, SSD_WIDTH, SSD_STATE), per_b),
                pl.BlockSpec((1, SSD_CONV - 1, SSD_CONV_DIM), per_b)]
    in_specs += [_const_spec(w.shape, 1) for w in lw]
    out_shape = [jax.ShapeDtypeStruct((bsz, CHUNK, D_MODEL), F32),
                 jax.ShapeDtypeStruct((bsz, SSD_WIDTH, SSD_STATE), F32),
                 jax.ShapeDtypeStruct((bsz, SSD_CONV - 1, SSD_CONV_DIM), F32),
                 jax.ShapeDtypeStruct((bsz, CHUNK, GMLP_WIDTH), F32)]
    out_specs = [pl.BlockSpec((1, CHUNK, D_MODEL), per_b),
                 pl.BlockSpec((1, SSD_WIDTH, SSD_STATE), per_b),
                 pl.BlockSpec((1, SSD_CONV - 1, SSD_CONV_DIM), per_b),
                 pl.BlockSpec((1, CHUNK, GMLP_WIDTH), per_b)]
    scratch = _tile_scratch(CHUNK) + [pltpu.VMEM((CHUNK, MIX_WIDTH), BF16),
                                      pltpu.VMEM((SSD_STATE, SSD_WIDTH), F32)]
    return pl.pallas_call(
        functools.partial(_mixer_state_kernel, valid=valid, alpha=alpha),
        grid=(bsz,),
        in_specs=in_specs,
        out_specs=out_specs,
        out_shape=out_shape,
        scratch_shapes=scratch,
        compiler_params=pltpu.CompilerParams(
            dimension_semantics=("parallel",),
            vmem_limit_bytes=VMEM_LIMIT_BYTES),
        name="mixer_state",
    )(x, ssm0, conv0, *lw)


def _ffn_kernel(x_ref, w1_ref, w2_ref, g_ref, b_ref, o_ref, *, alpha, fblk):
    x = x_ref[...]
    xb = x.astype(BF16)
    acc = jnp.zeros(x.shape, F32)
    for j in range(D_FF // fblk):
        h = _dot(xb, w1_ref[:, j * fblk:(j + 1) * fblk])
        h = jnp.square(jnp.maximum(h, 0.0)).astype(BF16)
        acc = acc + _dot(h, w2_ref[j * fblk:(j + 1) * fblk, :])
    o_ref[...] = _layer_norm(alpha * x + acc, g_ref[...], b_ref[...])


def _ffn_call(x2d, w1, w2, g, b, *, tile, alpha):
    rows = x2d.shape[0]
    return pl.pallas_call(
        functools.partial(_ffn_kernel, alpha=alpha, fblk=1024),
        grid=(rows // tile,),
        in_specs=[pl.BlockSpec((tile, D_MODEL), lambda i: (i, 0)),
                  _const_spec(w1.shape, 1), _const_spec(w2.shape, 1),
                  _const_spec(g.shape, 1), _const_spec(b.shape, 1)],
        out_specs=pl.BlockSpec((tile, D_MODEL), lambda i: (i, 0)),
        out_shape=jax.ShapeDtypeStruct((rows, D_MODEL), F32),
        compiler_params=pltpu.CompilerParams(
            dimension_semantics=("parallel",),
            vmem_limit_bytes=VMEM_LIMIT_BYTES),
        name="ffn",
    )(x2d, w1, w2, g, b)


def _pad_lanes(v):
    return jnp.pad(v, (0, LANES - v.shape[0]))[None, :]


def _to_chunk_rows(x):
    bsz, seq, d = x.shape
    x5 = x.reshape(bsz, seq // CHUNK, SUBLANES, ROW_GROUPS, d)
    return x5.swapaxes(2, 3).reshape(bsz, seq, d)


def _from_chunk_rows(x):
    bsz, seq, d = x.shape
    x5 = x.reshape(bsz, seq // CHUNK, ROW_GROUPS, SUBLANES, d)
    return x5.swapaxes(2, 3).reshape(bsz, seq, d)


def kernel(x_prompt, x_sample, state_ssm, state_conv, w_in, gmlp_ln_g, gmlp_ln_b, gmlp_ws, gmlp_bs,
           conv_w, conv_b, dt_bias, a_log, d_skip, ssd_norm_g, w_out, ln1_g, ln1_b, w_ff1, w_ff2,
           ln2_g, ln2_b):
    depth = w_in.shape[0]
    alpha = float((2 * depth) ** 0.25)
    bp, seq, _ = x_prompt.shape
    bs, dec_seq, _ = x_sample.shape
    row_tile = 512

    pos = jnp.array([_pos_of_row(p) for p in range(CHUNK)], jnp.int32)
    xp = _to_chunk_rows(x_prompt)
    xs = _to_chunk_rows(jnp.pad(x_sample, ((0, 0), (0, CHUNK - dec_seq), (0, 0))))
    ssm_p, conv_p, ssm_s, conv_s, v_s = [], [], [], [], []
    for l in range(depth):
        w_in_l = jnp.pad(w_in[l], ((0, 0), (0, D_IN_PAD - w_in.shape[2]))).astype(BF16)
        lw = MixerParams(
            w_in=w_in_l, w_out=w_out[l].astype(BF16),
            gln_g=gmlp_ln_g[l][None, :], gln_b=gmlp_ln_b[l][None, :],
            ws=gmlp_ws[l][:, pos][:, :, pos].astype(BF16), bst=gmlp_bs[l][:, pos].T,
            cw=conv_w[l], cb=conv_b[l][None, :],
            dtb=_pad_lanes(dt_bias[l]), alog=_pad_lanes(a_log[l]),
            dsk=jnp.repeat(d_skip[l], SSD_HEAD_DIM)[None, :], ng=ssd_norm_g[l][None, :],
            ln1g=ln1_g[l][None, :], ln1b=ln1_b[l][None, :])
        w1 = w_ff1[l].astype(BF16)
        w2 = w_ff2[l].astype(BF16)
        g2 = ln2_g[l][None, :]
        b2 = ln2_b[l][None, :]

        xp, sp, cp = _mixer_pipe_call(xp, lw, alpha=alpha)
        xs, ss, cs, vs = _mixer_state_call(xs, state_ssm[l].reshape(bs, SSD_WIDTH, SSD_STATE),
                                           state_conv[l], lw, valid=dec_seq, alpha=alpha)
        xp = _ffn_call(xp.reshape(bp * seq, D_MODEL), w1, w2, g2, b2,
                       tile=min(row_tile, bp * seq), alpha=alpha).reshape(bp, seq, D_MODEL)
        xs = _ffn_call(xs.reshape(bs * CHUNK, D_MODEL), w1, w2, g2, b2,
                       tile=min(row_tile, bs * CHUNK), alpha=alpha).reshape(bs, CHUNK, D_MODEL)
        ssm_p.append(sp.reshape(bp, SSD_HEADS, SSD_HEAD_DIM, SSD_STATE))
        conv_p.append(cp)
        ssm_s.append(ss.reshape(bs, SSD_HEADS, SSD_HEAD_DIM, SSD_STATE))
        conv_s.append(cs)
        v_s.append(_from_chunk_rows(vs)[:, :dec_seq])
    return (_from_chunk_rows(xp), _from_chunk_rows(xs)[:, :dec_seq], jnp.stack(ssm_p), jnp.stack(conv_p), jnp.stack(ssm_s), jnp.stack(conv_s),
            jnp.stack(v_s))
```
